```python
import math
import functools
import jax
import jax.numpy as jnp
from jax import lax
import numpy as np

D_MODEL = 4096
BATCH = 4
SEQ = 2048
DEPTH = 4
DEC_BATCH = 32
DEC_SEQ = 8
PAST_LEN = 8192
PAGE_SIZE = 128

HEAD_DIM = 64
N_HEADS = D_MODEL // HEAD_DIM
N_HEADS_A = N_HEADS // 2
N_HEADS_B = N_HEADS - N_HEADS_A
N_KV_A = N_HEADS_A // 4
N_KV_B = N_HEADS_B // 4
G_A = N_HEADS_A // N_KV_A
G_B = N_HEADS_B // N_KV_B
DILATIONS = ((128, 1), (512, 4), (2048, 16))
WIN_A = 2048
WIN_B = 128
Q_BLOCK = 128
N_BUCKETS = 32
MAX_DISTANCE = WIN_A
N_EXPERTS = 32
TOP_K = 4
D_EXPERT = D_MODEL // 4
SWIGLU_LIMIT = 7.0
SWIGLU_ALPHA = 1.702
NORM_EPS = 1e-5
QKV_SPLITS = (N_HEADS_A * HEAD_DIM, N_KV_A * HEAD_DIM, N_KV_A * HEAD_DIM,
              N_HEADS_B * HEAD_DIM, N_KV_B * HEAD_DIM, N_KV_B * HEAD_DIM)
QKV_COLS = sum(QKV_SPLITS)

kernel_name = 'hymba_dilated_swa_sink_moe_step'


def _rmsnorm(x, g):
    xf = x.astype(jnp.float32)
    y = xf * lax.rsqrt(jnp.mean(xf * xf, axis=-1, keepdims=True) + NORM_EPS)
    return (y * g.astype(jnp.float32)).astype(x.dtype)


def _t5_bucket(dist):
    max_exact = N_BUCKETS // 2
    d = jnp.maximum(dist, 1).astype(jnp.float32)
    large = max_exact + (jnp.log(d / max_exact) / math.log(MAX_DISTANCE / max_exact)
                         * (N_BUCKETS - max_exact)).astype(jnp.int32)
    large = jnp.minimum(large, N_BUCKETS - 1)
    return jnp.where(dist < max_exact, dist, large)


def _head_bias(table, n_kv, group):
    return table.T.reshape(n_kv, group, table.shape[0]).astype(jnp.float32)


def _split_heads(h, w_in):
    n, L, _ = h.shape
    offs = [int(o) for o in np.cumsum(QKV_SPLITS)[:-1]]
    qa, ka, va, qb, kb, vb = jnp.split(h @ w_in, offs, axis=-1)
    return (qa.reshape(n, L, N_KV_A, G_A, HEAD_DIM),
            ka.reshape(n, L, N_KV_A, HEAD_DIM), va.reshape(n, L, N_KV_A, HEAD_DIM),
            qb.reshape(n, L, N_KV_B, G_B, HEAD_DIM),
            kb.reshape(n, L, N_KV_B, HEAD_DIM), vb.reshape(n, L, N_KV_B, HEAD_DIM))


def _band_attn(q, k, v, bias_d, n_idx):
    n, L, hk, g, hd = q.shape
    nb = L // Q_BLOCK
    qb = q.reshape(n, nb, Q_BLOCK, hk, g, hd)

    def with_prev(a):
        ab = a.reshape(n, nb, Q_BLOCK, hk, hd)
        prev = jnp.pad(ab, ((0, 0), (1, 0), (0, 0), (0, 0), (0, 0)))[:, :-1]
        return jnp.concatenate([prev, ab], axis=2)

    kb, vb = with_prev(k), with_prev(v)
    s = jnp.einsum('nbqhgd,nbkhd->nbhgqk', qb, kb).astype(jnp.float32) * (hd ** -0.5)
    delta = Q_BLOCK + jnp.arange(Q_BLOCK)[:, None] - jnp.arange(2 * Q_BLOCK)[None, :]
    key_idx = jnp.arange(nb)[:, None] * Q_BLOCK - Q_BLOCK + jnp.arange(2 * Q_BLOCK)[None, :]
    ok = ((delta >= 0) & (delta <= n_idx))[None] & (key_idx >= 0)[:, None, :]
    bias = bias_d[:, :, jnp.clip(delta, 0, n_idx)]
    s = jnp.where(ok[None, :, None, None], s + bias, -jnp.inf)
    lse = jax.nn.logsumexp(s, axis=-1)
    p = jnp.exp(s - lse[..., None]).astype(v.dtype)
    o = jnp.einsum('nbhgqk,nbkhd->nbqhgd', p, vb).reshape(n, L, hk, g, hd)
    return o, jnp.moveaxis(lse, -1, 2).reshape(n, L, hk, g)


def _dilated_prompt(q, k, v, bias_d, r, n_idx):
    b, t = q.shape[:2]
    L = t // r
    Lp = -(-L // Q_BLOCK) * Q_BLOCK

    def split(a):
        a = a.reshape(b, L, r, *a.shape[2:])
        a = jnp.moveaxis(a, 2, 1).reshape(b * r, L, *a.shape[3:])
        return jnp.pad(a, [(0, 0), (0, Lp - L)] + [(0, 0)] * (a.ndim - 2))

    def merge(a):
        a = a[:, :L].reshape(b, r, L, *a.shape[2:])
        return jnp.moveaxis(a, 1, 2).reshape(b, t, *a.shape[3:])

    o, lse = _band_attn(split(q), split(k), split(v), bias_d, n_idx)
    return merge(o), merge(lse)


def _gather_attn(q, k_all, v_all, bias_d, r, n_idx):
    s_new, hd = q.shape[1], q.shape[-1]
    buf = k_all.shape[1] - s_new
    idx = buf + jnp.arange(s_new)[:, None] - r * jnp.arange(n_idx + 1)[None, :]
    ok = idx >= 0
    idx = jnp.maximum(idx, 0)
    kg, vg = k_all[:, idx], v_all[:, idx]
    s = jnp.einsum('nshgd,nsjhd->nhgsj', q, kg).astype(jnp.float32) * (hd ** -0.5)
    s = jnp.where(ok, s + bias_d[:, :, None, :], -jnp.inf)
    lse = jax.nn.logsumexp(s, axis=-1)
    p = jnp.exp(s - lse[..., None]).astype(v_all.dtype)
    o = jnp.einsum('nhgsj,nsjhd->nshgd', p, vg)
    return o, jnp.moveaxis(lse, -1, 1)


def _mix_by_denominator(outs):
    o = jnp.stack([a for a, _ in outs]).astype(jnp.float32)
    w = jax.nn.softmax(jnp.stack([l for _, l in outs]), axis=0)
    return jnp.sum(w[..., None] * o, axis=0)


def _apply_sink(o, lse, sink):
    return o.astype(jnp.float32) * jnp.exp(lse - jnp.logaddexp(lse, sink.astype(jnp.float32)))[..., None]


def _merge_heads(oa, ob, w_out, dtype):
    n, L = oa.shape[:2]
    o = jnp.concatenate([oa.reshape(n, L, -1), ob.reshape(n, L, -1)], axis=-1)
    return o.astype(dtype) @ w_out


def _mixer_prompt(h, w_in, w_out, sink, bias_a, bias_b):
    qa, ka, va, qb, kb, vb = _split_heads(h, w_in)
    oa = _mix_by_denominator([_dilated_prompt(qa, ka, va, bias_a[i], r, w // r)
                              for i, (w, r) in enumerate(DILATIONS)])
    ob = _apply_sink(*_dilated_prompt(qb, kb, vb, bias_b, 1, WIN_B), sink)
    y = _merge_heads(oa, ob, w_out, h.dtype)
    L = h.shape[1]
    na, nb = min(WIN_A, L), min(WIN_B, L)
    return y, (ka[:, L - na:], va[:, L - na:], kb[:, L - nb:], vb[:, L - nb:])


def _mixer_sample(h, ca_k, ca_v, cb_k, cb_v, w_in, w_out, sink, bias_a, bias_b):
    qa, ka, va, qb, kb, vb = _split_heads(h, w_in)
    ka_all = jnp.concatenate([ca_k, ka], axis=1)
    va_all = jnp.concatenate([ca_v, va], axis=1)
    kb_all = jnp.concatenate([cb_k, kb], axis=1)
    vb_all = jnp.concatenate([cb_v, vb], axis=1)
    oa = _mix_by_denominator([_gather_attn(qa, ka_all, va_all, bias_a[i], r, w // r)
                              for i, (w, r) in enumerate(DILATIONS)])
    ob = _apply_sink(*_gather_attn(qb, kb_all, vb_all, bias_b, 1, WIN_B), sink)
    y = _merge_heads(oa, ob, w_out, h.dtype)
    na, nb = min(WIN_A, ka_all.shape[1]), min(WIN_B, kb_all.shape[1])
    return y, (ka_all[:, -na:], va_all[:, -na:], kb_all[:, -nb:], vb_all[:, -nb:])


def _clamped_swiglu(u):
    x_glu = jnp.minimum(u[..., ::2], SWIGLU_LIMIT)
    x_lin = jnp.clip(u[..., 1::2], -SWIGLU_LIMIT, SWIGLU_LIMIT)
    return x_glu * jax.nn.sigmoid(SWIGLU_ALPHA * x_glu) * (x_lin + 1)


def _moe(h, w_router, b_router, w1, b1, w2, b2):
    n, L, d = h.shape
    t = h.reshape(n * L, d)
    logits = (t @ w_router + b_router).astype(jnp.float32)
    top_v, top_i = lax.top_k(logits, TOP_K)
    gates = jax.nn.softmax(top_v, axis=-1)
    combine = jnp.einsum('tk,tke->et', gates, jax.nn.one_hot(top_i, N_EXPERTS, dtype=jnp.float32))

    def expert(acc, params):
        w1_e, b1_e, w2_e, b2_e, gate_e = params
        y_e = _clamped_swiglu(t @ w1_e + b1_e) @ w2_e + b2_e
        return acc + gate_e[:, None] * y_e.astype(jnp.float32), None

    y, _ = lax.scan(expert, jnp.zeros((n * L, d), jnp.float32), (w1, b1, w2, b2, combine))
    return y.astype(h.dtype).reshape(n, L, d)


def _modulation(c, w_ada, b_ada):
    mod = jax.nn.silu(c) @ w_ada + b_ada
    return jnp.split(mod[:, None, :], 6, axis=-1)


def _layer(x, c, mix_fn, w_ada, b_ada, g_mix, g_ffn, w_router, b_router, w1, b1, w2, b2):
    sh1, sc1, gt1, sh2, sc2, gt2 = _modulation(c, w_ada, b_ada)
    y, rows = mix_fn(_rmsnorm(x, g_mix) * (1 + sc1) + sh1)
    x = x + gt1 * y
    x = x + gt2 * _moe(_rmsnorm(x, g_ffn) * (1 + sc2) + sh2, w_router, b_router, w1, b1, w2, b2)
    return x, rows


def setup_inputs(seed: int = 0) -> dict:
    key = jax.random.key(seed)
    ks = jax.random.split(key, 23)
    f32 = jnp.float32

    def nrm(k, shape, scale=1.0):
        return jax.random.normal(k, shape, f32) * scale

    buf_a = min(WIN_A, PAST_LEN)
    buf_b = min(WIN_B, PAST_LEN)
    return {
        'x_prompt': nrm(ks[0], (BATCH, SEQ, D_MODEL)),
        'x_sample': nrm(ks[1], (DEC_BATCH, DEC_SEQ, D_MODEL)),
        'cache_a_k': nrm(ks[2], (DEPTH, DEC_BATCH, buf_a, N_KV_A, HEAD_DIM)),
        'cache_a_v': nrm(ks[3], (DEPTH, DEC_BATCH, buf_a, N_KV_A, HEAD_DIM)),
        'cache_b_k': nrm(ks[4], (DEPTH, DEC_BATCH, buf_b, N_KV_B, HEAD_DIM)),
        'cache_b_v': nrm(ks[5], (DEPTH, DEC_BATCH, buf_b, N_KV_B, HEAD_DIM)),
        'c_prompt': nrm(ks[6], (BATCH, D_MODEL)),
        'c_sample': nrm(ks[7], (DEC_BATCH, D_MODEL)),
        'rel_bias': nrm(ks[8], (N_BUCKETS, N_HEADS), 0.5),
        'w_ada': nrm(ks[9], (DEPTH, D_MODEL, 6 * D_MODEL), 0.5 * D_MODEL ** -0.5),
        'b_ada': nrm(ks[10], (DEPTH, 6 * D_MODEL), 0.02),
        'g_mix': 1.0 + nrm(ks[11], (DEPTH, D_MODEL), 0.05),
        'g_ffn': 1.0 + nrm(ks[12], (DEPTH, D_MODEL), 0.05),
        'w_in': nrm(ks[13], (DEPTH, D_MODEL, QKV_COLS), D_MODEL ** -0.5),
        'w_out': nrm(ks[14], (DEPTH, N_HEADS * HEAD_DIM, D_MODEL), (N_HEADS * HEAD_DIM) ** -0.5),
        'sinks': nrm(ks[15], (DEPTH, N_HEADS_B), 0.5),
        'w_router': nrm(ks[16], (DEPTH, D_MODEL, N_EXPERTS), D_MODEL ** -0.5),
        'b_router': nrm(ks[17], (DEPTH, N_EXPERTS), 0.01),
        'w1': nrm(ks[18], (DEPTH, N_EXPERTS, D_MODEL, 2 * D_EXPERT), D_MODEL ** -0.5),
        'b1': nrm(ks[19], (DEPTH, N_EXPERTS, 2 * D_EXPERT), 0.02),
        'w2': nrm(ks[20], (DEPTH, N_EXPERTS, D_EXPERT, D_MODEL), D_EXPERT ** -0.5),
        'b2': nrm(ks[21], (DEPTH, N_EXPERTS, D_MODEL), 0.02),
        'g_final': 1.0 + nrm(ks[22], (D_MODEL,), 0.05),
    }


def reference(x_prompt, x_sample, cache_a_k, cache_a_v, cache_b_k, cache_b_v, c_prompt, c_sample,
              rel_bias, w_ada, b_ada, g_mix, g_ffn, w_in, w_out, sinks, w_router, b_router,
              w1, b1, w2, b2, g_final):
    bias_dist = rel_bias[_t5_bucket(jnp.arange(MAX_DISTANCE + 1))]
    bias_a = [_head_bias(bias_dist[r * jnp.arange(w // r + 1), :N_HEADS_A], N_KV_A, G_A)
              for w, r in DILATIONS]
    bias_b = _head_bias(bias_dist[:WIN_B + 1, N_HEADS_A:], N_KV_B, G_B)

    xp, xs = x_prompt, x_sample
    prompt_rows, sample_rows = [], []
    for l in range(DEPTH):
        sink = sinks[l].reshape(N_KV_B, G_B)
        ffn = (w_router[l], b_router[l], w1[l], b1[l], w2[l], b2[l])
        mix_p = functools.partial(_mixer_prompt, w_in=w_in[l], w_out=w_out[l], sink=sink,
                                  bias_a=bias_a, bias_b=bias_b)
        xp, rows_p = _layer(xp, c_prompt, mix_p, w_ada[l], b_ada[l], g_mix[l], g_ffn[l], *ffn)
        mix_s = functools.partial(_mixer_sample, ca_k=cache_a_k[l], ca_v=cache_a_v[l],
                                  cb_k=cache_b_k[l], cb_v=cache_b_v[l], w_in=w_in[l],
                                  w_out=w_out[l], sink=sink, bias_a=bias_a, bias_b=bias_b)
        xs, rows_s = _layer(xs, c_sample, mix_s, w_ada[l], b_ada[l], g_mix[l], g_ffn[l], *ffn)
        prompt_rows.append(rows_p)
        sample_rows.append(rows_s)

    prompt_a_k, prompt_a_v, prompt_b_k, prompt_b_v = [jnp.stack(r) for r in zip(*prompt_rows)]
    sample_a_k, sample_a_v, sample_b_k, sample_b_v = [jnp.stack(r) for r in zip(*sample_rows)]
    y_prompt = _rmsnorm(xp, g_final)
    y_sample = _rmsnorm(xs, g_final)
    return (y_prompt, y_sample, prompt_a_k, prompt_a_v, prompt_b_k, prompt_b_v,
            sample_a_k, sample_a_v, sample_b_k, sample_b_v)
```

```python
import functools
import math

import numpy as np
import jax
import jax.numpy as jnp
from jax import lax
from jax.experimental import pallas as pl
from jax.experimental.pallas import tpu as pltpu

F32 = jnp.float32
BF16 = jnp.bfloat16
I32 = jnp.int32

HEAD_DIM = 64
KV_GROUP = 4
Q_BLOCK = 128
DILATIONS = ((128, 1), (512, 4), (2048, 16))
WIN_B = 128
N_BUCKETS = 32
MAX_DISTANCE = 2048
TOP_K = 4
SWIGLU_LIMIT = 7.0
SWIGLU_ALPHA = 1.702
NORM_EPS = 1e-5
MASKED = -1e30
REMOVED = -3e38

LANES = 128
SUBLANES = 8
VMEM_LIMIT = 56 * 1024 * 1024

ROW_TILE = 256
MM_TM = 512
MM_TN = 512
SUB = 256
SUBS_PER_SLOT = 5
CAP = SUB * SUBS_PER_SLOT
UP_CHUNK = 512
DOWN_CHUNK = 1024
COMB_TILE = 128
ID_ALIGN = 1024
SLOT_IDS = -(-CAP // ID_ALIGN) * ID_ALIGN
COMB_IDS = -(-(TOP_K * COMB_TILE) // ID_ALIGN) * ID_ALIGN


def _cparams(*sem):
    return pltpu.CompilerParams(dimension_semantics=sem, vmem_limit_bytes=VMEM_LIMIT)


def _dot(a, b):
    return jnp.dot(a, b, preferred_element_type=F32)


def _dot_nt(a, b):
    return lax.dot_general(a, b, (((1,), (1,)), ((), ())), preferred_element_type=F32)


def _split_bf16(x):
    hi = x.astype(BF16)
    lo = (x - hi.astype(F32)).astype(BF16)
    return hi, lo


def _mod_kernel(c_ref, w_ref, b_ref, o_ref):
    c = c_ref[...]
    a = (c / (1.0 + jnp.exp(-c))).astype(BF16)
    o_ref[...] = _dot(a, w_ref[...].astype(BF16)) + b_ref[...]


def _modulation(c_all, w_ada, b_ada):
    depth, d, n6 = w_ada.shape
    r = c_all.shape[0]
    tn = 512
    return pl.pallas_call(
        _mod_kernel,
        grid=(depth, n6 // tn),
        in_specs=[pl.BlockSpec((r, d), lambda l, j: (0, 0)),
                  pl.BlockSpec((None, d, tn), lambda l, j: (l, 0, j)),
                  pl.BlockSpec((None, 1, tn), lambda l, j: (l, 0, j))],
        out_specs=pl.BlockSpec((None, r, tn), lambda l, j: (l, 0, j)),
        out_shape=jax.ShapeDtypeStruct((depth, r, n6), F32),
        compiler_params=_cparams("arbitrary", "arbitrary"),
        name="modulation",
    )(c_all, w_ada, b_ada.reshape(depth, 1, n6))


class _Group:
    def __init__(self, n_tokens, tile, mods, tiles_per_blk):
        self.n = n_tokens
        self.tile = tile
        self.mods = mods
        self.tiles_per_blk = tiles_per_blk

    def mod_spec(self, k, width=None, tile=None, col=None):
        nblk, rows, d = self.mods[k].shape
        width = d if width is None else width
        tile = self.tile if tile is None else tile
        if rows == 1:
            tpb = self.tiles_per_blk * self.tile // tile
            if col is None:
                return pl.BlockSpec((None, 1, width), lambda i: (i // tpb, 0, 0))
            return pl.BlockSpec((None, 1, width), lambda j, i: (i // tpb, 0, j))
        if col is None:
            return pl.BlockSpec((None, tile, width), lambda i: (0, i, 0))
        return pl.BlockSpec((None, tile, width), lambda j, i: (0, i, j))


def _rms(x, g):
    return x * lax.rsqrt(jnp.mean(x * x, axis=-1, keepdims=True) + NORM_EPS) * g


def _norm_kernel(x_ref, g_ref, sc_ref, sh_ref, o_ref):
    y = _rms(x_ref[...], g_ref[...]) * (1.0 + sc_ref[...]) + sh_ref[...]
    o_ref[...] = y.astype(o_ref.dtype)


def _norm_plain_kernel(x_ref, g_ref, o_ref):
    o_ref[...] = _rms(x_ref[...], g_ref[...]).astype(o_ref.dtype)


def _norm_mod(x, g, grp, sc_k, sh_k, out_dtype):
    t, d = x.shape
    tr = grp.tile
    return pl.pallas_call(
        _norm_kernel,
        grid=(t // tr,),
        in_specs=[pl.BlockSpec((tr, d), lambda i: (i, 0)),
                  pl.BlockSpec((1, d), lambda i: (0, 0)),
                  grp.mod_spec(sc_k), grp.mod_spec(sh_k)],
        out_specs=pl.BlockSpec((tr, d), lambda i: (i, 0)),
        out_shape=jax.ShapeDtypeStruct((t, d), out_dtype),
        compiler_params=_cparams("arbitrary"),
        name="norm_mod",
    )(x, g.reshape(1, d), grp.mods[sc_k], grp.mods[sh_k])


def _norm_plain(x, g, tr):
    t, d = x.shape
    return pl.pallas_call(
        _norm_plain_kernel,
        grid=(t // tr,),
        in_specs=[pl.BlockSpec((tr, d), lambda i: (i, 0)),
                  pl.BlockSpec((1, d), lambda i: (0, 0))],
        out_specs=pl.BlockSpec((tr, d), lambda i: (i, 0)),
        out_shape=jax.ShapeDtypeStruct((t, d), F32),
        compiler_params=_cparams("arbitrary"),
        name="norm_final",
    )(x, g.reshape(1, d))


def _mm_kernel(*refs, n_a, has_res, scale):
    a_refs = refs[:n_a]
    w_ref = refs[n_a]
    pos = n_a + 1
    if has_res:
        res_ref, gate_ref = refs[pos], refs[pos + 1]
        pos += 2
    o_ref, wbf_ref = refs[pos], refs[pos + 1]

    @pl.when(pl.program_id(1) == 0)
    def _():
        wbf_ref[...] = w_ref[...].astype(BF16)

    acc = None
    k0 = 0
    for a_ref in a_refs:
        kk = a_ref.shape[1]
        part = _dot(a_ref[...], wbf_ref[k0:k0 + kk, :])
        acc = part if acc is None else acc + part
        k0 += kk
    if scale != 1.0:
        acc = acc * scale
    if has_res:
        acc = res_ref[...] + gate_ref[...] * acc
    o_ref[...] = acc.astype(o_ref.dtype)


def _matmul_ws(a_list, w, layer, col0, ncols, out_dtype, tm, scale=1.0, res=None, grp=None, gate_k=None):
    m = a_list[0].shape[0]
    ktot = w.shape[1]
    tn = math.gcd(MM_TN, ncols, col0)
    assert tn % LANES == 0
    cb0 = col0 // tn
    in_specs = [pl.BlockSpec((tm, a.shape[1]), lambda j, i: (i, 0)) for a in a_list]
    in_specs.append(pl.BlockSpec((None, ktot, tn), lambda j, i: (layer, 0, cb0 + j)))
    args = list(a_list) + [w]
    if res is not None:
        in_specs.append(pl.BlockSpec((tm, tn), lambda j, i: (i, j)))
        in_specs.append(grp.mod_spec(gate_k, width=tn, tile=tm, col=True))
        args += [res, grp.mods[gate_k]]
    kern = functools.partial(_mm_kernel, n_a=len(a_list), has_res=res is not None, scale=scale)
    return pl.pallas_call(
        kern,
        grid=(ncols // tn, m // tm),
        in_specs=in_specs,
        out_specs=pl.BlockSpec((tm, tn), lambda j, i: (i, j)),
        out_shape=jax.ShapeDtypeStruct((m, ncols), out_dtype),
        scratch_shapes=[pltpu.VMEM((ktot, tn), BF16)],
        compiler_params=_cparams("arbitrary", "arbitrary"),
        name="matmul_ws",
    )(*args)


def _t5_bucket(dist):
    max_exact = N_BUCKETS // 2
    d = jnp.maximum(dist, 1).astype(F32)
    large = max_exact + (jnp.log(d / max_exact) / math.log(MAX_DISTANCE / max_exact)
                         * (N_BUCKETS - max_exact)).astype(I32)
    large = jnp.minimum(large, N_BUCKETS - 1)
    return jnp.where(dist < max_exact, dist, large)


def _table_kernel(rb_ref, oh_ref, c_ref, o_ref):
    hi, lo = _split_bf16(rb_ref[...])
    oh = oh_ref[...]
    o_ref[...] = _dot(hi, oh) + _dot(lo, oh) + c_ref[...]


def _bias_table(rel_bias_t, dist, add):
    n = dist.shape[0]
    nh = rel_bias_t.shape[0]
    bucket = _t5_bucket(jnp.asarray(np.maximum(dist, 0), I32))
    onehot = (bucket[None, :] == jnp.arange(N_BUCKETS, dtype=I32)[:, None]).astype(BF16)
    tn = min(n, 8192)
    assert n % tn == 0
    return pl.pallas_call(
        _table_kernel,
        grid=(n // tn,),
        in_specs=[pl.BlockSpec((nh, N_BUCKETS), lambda j: (0, 0)),
                  pl.BlockSpec((N_BUCKETS, tn), lambda j: (0, j)),
                  pl.BlockSpec((1, tn), lambda j: (0, j))],
        out_specs=pl.BlockSpec((nh, tn), lambda j: (0, j)),
        out_shape=jax.ShapeDtypeStruct((nh, n), F32),
        compiler_params=_cparams("arbitrary"),
        name="bias_table",
    )(rel_bias_t, onehot, jnp.asarray(add, F32).reshape(1, n))


def _band_table(rel_bias_t, r, heads):
    q = np.arange(Q_BLOCK)[:, None]
    kc = np.arange(2 * Q_BLOCK)[None, :]
    j = Q_BLOCK + q - kc
    ok = (j >= 0) & (j <= Q_BLOCK)
    dist = np.where(ok, r * j, 0).reshape(-1)
    add = np.where(ok, 0.0, MASKED).reshape(-1)
    tab = _bias_table(rel_bias_t, dist, add)[heads]
    nkv = tab.shape[0] // KV_GROUP
    tab = tab.reshape(nkv // 2, 2, 2, 2, Q_BLOCK, 2 * Q_BLOCK)
    tab = tab.transpose(0, 1, 3, 2, 4, 5)
    return tab.reshape(nkv // 2, 2, 2, 2 * Q_BLOCK, 2 * Q_BLOCK)


def _decode_tables(rel_bias_t, heads, n_new, buf, branches):
    s = np.arange(n_new)[:, None]

    def mult(d):
        m = np.zeros(d.shape, np.int64)
        for w, r in branches:
            m += ((d >= 0) & (d % r == 0) & (d // r <= w // r)).astype(np.int64)
        return m

    def build(d):
        m = mult(d)
        dist = np.where(m > 0, d, 0).reshape(-1)
        add = np.where(m > 0, np.log(np.maximum(m, 1)), MASKED).reshape(-1)
        tab = _bias_table(rel_bias_t, dist, add)[heads]
        return tab.reshape(tab.shape[0] * n_new, d.shape[1])

    d_cache = buf + s - np.arange(buf)[None, :]
    d_new = s - np.arange(LANES)[None, :]
    d_new = np.where(np.arange(LANES)[None, :] < n_new, d_new, -1)
    return build(d_cache), build(d_new)


def _band_kernel(*refs, ls, has_sink, emit_lse):
    if has_sink:
        sink_ref, refs = refs[0], refs[1:]
    q_ref, k_ref, v_ref, t_ref = refs[:4]
    o_ref = refs[4]
    pos = 5
    if emit_lse:
        lse_ref = refs[pos]
        pos += 1
    kz_ref, vz_ref = refs[pos], refs[pos + 1]
    hp = pl.program_id(0)
    nqb = ls // Q_BLOCK
    half = HEAD_DIM

    lane = lax.broadcasted_iota(I32, (ls, LANES), 1)
    low = lane < half
    kf = k_ref[...]
    vf = v_ref[...]
    for kvh in range(2):
        for src, dst in ((kf, kz_ref), (vf, vz_ref)):
            if kvh == 0:
                a_lo = jnp.where(low, src, 0.0)
                a_hi = pltpu.roll(a_lo, half, 1)
            else:
                a_hi = jnp.where(low, 0.0, src)
                a_lo = pltpu.roll(a_hi, half, 1)
            dst[kvh, 0] = a_lo.astype(BF16)
            dst[kvh, 1] = a_hi.astype(BF16)

    lane_q = lax.broadcasted_iota(I32, (2 * Q_BLOCK, LANES), 1)
    low_q = lane_q < half
    row_q = lax.broadcasted_iota(I32, (2 * Q_BLOCK, 1), 0)
    lane_l = lax.broadcasted_iota(I32, (Q_BLOCK, LANES), 1)

    def block(r0, k0, nk, tcol0):
        lse_tile = jnp.zeros((Q_BLOCK, LANES), F32)
        for kvh in range(2):
            c0 = kvh * KV_GROUP * HEAD_DIM
            lhs = jnp.concatenate([q_ref[pl.ds(r0, Q_BLOCK), c0:c0 + LANES],
                                   q_ref[pl.ds(r0, Q_BLOCK), c0 + LANES:c0 + 2 * LANES]], axis=0)
            outs = None
            stats = []
            for par in range(2):
                s = _dot_nt(lhs, kz_ref[kvh, par, pl.ds(k0, nk), :])
                s = s + t_ref[kvh, par, :, tcol0:tcol0 + nk]
                m = jnp.max(s, axis=-1, keepdims=True)
                if has_sink:
                    hb = (hp * 2 + kvh) * KV_GROUP
                    sk = jnp.where(row_q < Q_BLOCK, sink_ref[hb + par], sink_ref[hb + 2 + par])
                    m = jnp.maximum(m, sk)
                p = jnp.exp(s - m)
                l = jnp.sum(p, axis=-1, keepdims=True)
                if has_sink:
                    l = l + jnp.exp(sk - m)
                pv = _dot(p.astype(BF16), vz_ref[kvh, par, pl.ds(k0, nk), :])
                outs = pv if outs is None else outs + pv
                stats.append((m, l))
            inv = jnp.where(low_q, 1.0 / stats[0][1], 1.0 / stats[1][1])
            o = (outs * inv).astype(o_ref.dtype)
            o_ref[pl.ds(r0, Q_BLOCK), c0:c0 + LANES] = o[:Q_BLOCK]
            o_ref[pl.ds(r0, Q_BLOCK), c0 + LANES:c0 + 2 * LANES] = o[Q_BLOCK:]
            if emit_lse:
                for par in range(2):
                    lse = stats[par][0] + jnp.log(stats[par][1])
                    for j in range(2):
                        g = 2 * j + par
                        lse_tile = jnp.where(lane_l == kvh * KV_GROUP + g,
                                             lse[j * Q_BLOCK:(j + 1) * Q_BLOCK], lse_tile)
        if emit_lse:
            lse_ref[pl.ds(r0, Q_BLOCK), :] = lse_tile

    block(0, 0, Q_BLOCK, Q_BLOCK)
    if nqb > 1:
        def body(qb, carry):
            r0 = pl.multiple_of(qb * Q_BLOCK, Q_BLOCK)
            k0 = pl.multiple_of(qb * Q_BLOCK - Q_BLOCK, Q_BLOCK)
            block(r0, k0, 2 * Q_BLOCK, 0)
            return carry
        lax.fori_loop(1, nqb, body, 0)


def _band_attention(q, k, v, table, r, sinks=None, emit_lse=True):
    b, l, qw = q.shape
    kw = k.shape[2]
    nhp = kw // LANES
    ls = l // r
    qs = q.reshape(b, ls, r * qw)
    ks = k.reshape(b, ls, r * kw)
    vs = v.reshape(b, ls, r * kw)
    qblk = qw // nhp
    has_sink = sinks is not None

    def im(hp, bi, c, *_):
        return (bi, 0, c * nhp + hp)

    in_specs = [pl.BlockSpec((None, ls, qblk), im),
                pl.BlockSpec((None, ls, LANES), im),
                pl.BlockSpec((None, ls, LANES), im),
                pl.BlockSpec((None, 2, 2, 2 * Q_BLOCK, 2 * Q_BLOCK), lambda hp, bi, c, *_: (hp, 0, 0, 0, 0))]
    out_specs = [pl.BlockSpec((None, ls, qblk), im)]
    out_shape = [jax.ShapeDtypeStruct((b, ls, r * qw), BF16)]
    if emit_lse:
        out_specs.append(pl.BlockSpec((None, ls, LANES), im))
        out_shape.append(jax.ShapeDtypeStruct((b, ls, r * nhp * LANES), F32))
    kern = functools.partial(_band_kernel, ls=ls, has_sink=has_sink, emit_lse=emit_lse)
    gs = pltpu.PrefetchScalarGridSpec(
        num_scalar_prefetch=1 if has_sink else 0,
        grid=(nhp, b, r),
        in_specs=in_specs,
        out_specs=out_specs,
        scratch_shapes=[pltpu.VMEM((2, 2, ls, LANES), BF16), pltpu.VMEM((2, 2, ls, LANES), BF16)])
    args = ([sinks] if has_sink else []) + [qs, ks, vs, table]
    outs = pl.pallas_call(
        kern, grid_spec=gs, out_shape=out_shape,
        compiler_params=_cparams("arbitrary", "arbitrary", "arbitrary"),
        name="band_attention",
    )(*args)
    o = outs[0].reshape(b, l, qw)
    if emit_lse:
        return o, outs[1].reshape(b, l, nhp * LANES)
    return o


def _mix_kernel(o1_ref, o2_ref, o3_ref, l1_ref, l2_ref, l3_ref, e_ref, out_ref):
    ls = [l1_ref[...], l2_ref[...], l3_ref[...]]
    m = jnp.maximum(jnp.maximum(ls[0], ls[1]), ls[2])
    ws = [jnp.exp(x - m) for x in ls]
    inv = 1.0 / (ws[0] + ws[1] + ws[2])
    e = e_ref[...]
    acc = None
    for w, o_ref in zip(ws, (o1_ref, o2_ref, o3_ref)):
        hi, lo = _split_bf16(w * inv)
        wide = _dot(hi, e) + _dot(lo, e)
        part = wide * o_ref[...].astype(F32)
        acc = part if acc is None else acc + part
    out_ref[...] = acc.astype(out_ref.dtype)


def _mix_branches(outs, lses):
    b, l, qw = outs[0].shape
    nhp = lses[0].shape[2] // LANES
    qblk = qw // nhp
    heads_per = qblk // HEAD_DIM
    expand = (np.arange(LANES)[:, None] == (np.arange(qblk)[None, :] // HEAD_DIM)) & \
             (np.arange(LANES)[:, None] < heads_per)
    tr = 512
    o_spec = pl.BlockSpec((None, tr, qblk), lambda bi, i, hp: (bi, i, hp))
    l_spec = pl.BlockSpec((None, tr, LANES), lambda bi, i, hp: (bi, i, hp))
    return pl.pallas_call(
        _mix_kernel,
        grid=(b, l // tr, nhp),
        in_specs=[o_spec] * 3 + [l_spec] * 3 + [pl.BlockSpec((LANES, qblk), lambda bi, i, hp: (0, 0))],
        out_specs=o_spec,
        out_shape=jax.ShapeDtypeStruct((b, l, qw), BF16),
        compiler_params=_cparams("arbitrary", "arbitrary", "arbitrary"),
        name="mix_branches",
    )(*outs, *lses, jnp.asarray(expand, BF16))


def _decode_kernel(*refs, has_sink):
    q_ref, kc_ref, vc_ref, kn_ref, vn_ref, tc_ref, tn_ref = refs[:7]
    pos = 7
    if has_sink:
        sink_ref = refs[pos]
        pos += 1
    o_ref = refs[pos]
    q = q_ref[...]
    s_c = _dot_nt(q, kc_ref[...].astype(BF16)) + tc_ref[...]
    s_n = _dot_nt(q, kn_ref[...].astype(BF16)) + tn_ref[...]
    m = jnp.maximum(jnp.max(s_c, axis=-1, keepdims=True), jnp.max(s_n, axis=-1, keepdims=True))
    if has_sink:
        sk = sink_ref[:, 0:1]
        m = jnp.maximum(m, sk)
    p_c = jnp.exp(s_c - m)
    p_n = jnp.exp(s_n - m)
    l = jnp.sum(p_c, axis=-1, keepdims=True) + jnp.sum(p_n, axis=-1, keepdims=True)
    if has_sink:
        l = l + jnp.exp(sk - m)
    o = _dot(p_c.astype(BF16), vc_ref[...].astype(BF16)) + _dot(p_n.astype(BF16), vn_ref[...].astype(BF16))
    o_ref[...] = o / l


def _decode_attention(q, cache_k, cache_v, layer, k_new, v_new, t_cache, t_new, sink_col=None):
    n, s_new, qw = q.shape
    kw = k_new.shape[2]
    nkv = kw // HEAD_DIM
    buf = cache_k.shape[2]
    rows = nkv * KV_GROUP * s_new
    q6 = q.reshape(n, s_new, nkv, KV_GROUP, HEAD_DIM).transpose(0, 2, 3, 1, 4)
    eye = jnp.eye(nkv, dtype=q.dtype)
    qbd = (q6[:, :, :, :, None, :] * eye[None, :, None, None, :, None]).reshape(n, rows, kw)
    pad = LANES - s_new
    kn = jnp.pad(k_new, ((0, 0), (0, pad), (0, 0)))
    vn = jnp.pad(v_new, ((0, 0), (0, pad), (0, 0)))
    has_sink = sink_col is not None
    in_specs = [pl.BlockSpec((None, rows, kw), lambda i: (i, 0, 0)),
                pl.BlockSpec((None, None, buf, kw), lambda i: (layer, i, 0, 0)),
                pl.BlockSpec((None, None, buf, kw), lambda i: (layer, i, 0, 0)),
                pl.BlockSpec((None, LANES, kw), lambda i: (i, 0, 0)),
                pl.BlockSpec((None, LANES, kw), lambda i: (i, 0, 0)),
                pl.BlockSpec((rows, buf), lambda i: (0, 0)),
                pl.BlockSpec((rows, LANES), lambda i: (0, 0))]
    args = [qbd, cache_k, cache_v, kn, vn, t_cache, t_new]
    if has_sink:
        in_specs.append(pl.BlockSpec((rows, LANES), lambda i: (0, 0)))
        args.append(sink_col)
    o = pl.pallas_call(
        functools.partial(_decode_kernel, has_sink=has_sink),
        grid=(n,),
        in_specs=in_specs,
        out_specs=pl.BlockSpec((None, rows, kw), lambda i: (i, 0, 0)),
        out_shape=jax.ShapeDtypeStruct((n, rows, kw), F32),
        compiler_params=_cparams("arbitrary"),
        name="decode_attention",
    )(*args)
    o6 = o.reshape(n, nkv, KV_GROUP * s_new, nkv, HEAD_DIM)
    od = jnp.einsum('nhxhd->nhxd', o6).reshape(n, nkv, KV_GROUP, s_new, HEAD_DIM)
    return od.transpose(0, 3, 1, 2, 4).reshape(n, s_new, qw)


def _route_kernel(x_ref, g_ref, sc_ref, sh_ref, wr_ref, br_ref, cin_ref,
                  h_ref, eid_ref, gate_ref, rank_ref, cnt_ref, run_ref):
    i = pl.program_id(0)

    @pl.when(i == 0)
    def _():
        run_ref[...] = cin_ref[0:1, :].astype(F32)

    h = _rms(x_ref[...], g_ref[...]) * (1.0 + sc_ref[...]) + sh_ref[...]
    h_ref[...] = h
    h_hi, h_lo = _split_bf16(h)
    w_hi, w_lo = _split_bf16(wr_ref[...])
    logits = _dot(h_hi, w_hi) + _dot(h_hi, w_lo) + _dot(h_lo, w_hi) + br_ref[...]
    tr = logits.shape[0]
    lane_i = lax.broadcasted_iota(I32, (tr, LANES), 1)
    lane = lane_i.astype(F32)
    cur = logits
    tops, idxs, sels = [], [], []
    for _k in range(TOP_K):
        m = jnp.max(cur, axis=-1, keepdims=True)
        idx = jnp.min(jnp.where(cur == m, lane, float(LANES)), axis=-1, keepdims=True)
        sel = lane == idx
        cur = jnp.where(sel, REMOVED, cur)
        tops.append(m)
        idxs.append(idx)
        sels.append(sel)
    es = [jnp.exp(t - tops[0]) for t in tops]
    inv = 1.0 / (es[0] + es[1] + es[2] + es[3])
    selmask = jnp.where(sels[0] | sels[1] | sels[2] | sels[3], 1.0, 0.0)
    r_i = lax.broadcasted_iota(I32, (tr, tr), 0)
    c_i = lax.broadcasted_iota(I32, (tr, tr), 1)
    ltri = jnp.where(r_i > c_i, 1.0, 0.0).astype(BF16)
    rank_dense = run_ref[...] + _dot(ltri, selmask.astype(BF16))
    eid = jnp.zeros((tr, LANES), F32)
    gate = jnp.zeros((tr, LANES), F32)
    rank = jnp.zeros((tr, LANES), F32)
    for k in range(TOP_K):
        rk = jnp.sum(jnp.where(sels[k], rank_dense, 0.0), axis=-1, keepdims=True)
        eid = jnp.where(lane_i == k, idxs[k], eid)
        gate = jnp.where(lane_i == k, es[k] * inv, gate)
        rank = jnp.where(lane_i == k, rk, rank)
    eid_ref[...] = eid.astype(I32)
    gate_ref[...] = gate
    rank_ref[...] = rank.astype(I32)
    run = run_ref[...] + jnp.sum(selmask, axis=0, keepdims=True)
    run_ref[...] = run
    cnt_ref[...] = jnp.broadcast_to(run, cnt_ref.shape).astype(I32)


def _route(x, g, grp, w_router_pad, b_router_pad, cnt_in):
    t, d = x.shape
    tr = grp.tile
    row = lambda i: (i, 0)
    fixed = lambda i: (0, 0)
    return pl.pallas_call(
        _route_kernel,
        grid=(t // tr,),
        in_specs=[pl.BlockSpec((tr, d), row),
                  pl.BlockSpec((1, d), fixed),
                  grp.mod_spec(4), grp.mod_spec(3),
                  pl.BlockSpec((d, LANES), fixed),
                  pl.BlockSpec((1, LANES), fixed),
                  pl.BlockSpec((SUBLANES, LANES), fixed)],
        out_specs=[pl.BlockSpec((tr, d), row),
                   pl.BlockSpec((tr, LANES), row),
                   pl.BlockSpec((tr, LANES), row),
                   pl.BlockSpec((tr, LANES), row),
                   pl.BlockSpec((SUBLANES, LANES), fixed)],
        out_shape=[jax.ShapeDtypeStruct((t, d), F32),
                   jax.ShapeDtypeStruct((t, LANES), I32),
                   jax.ShapeDtypeStruct((t, LANES), F32),
                   jax.ShapeDtypeStruct((t, LANES), I32),
                   jax.ShapeDtypeStruct((SUBLANES, LANES), I32)],
        scratch_shapes=[pltpu.VMEM((1, LANES), F32)],
        compiler_params=_cparams("arbitrary"),
        name="moe_route",
    )(x, g.reshape(1, d), grp.mods[4], grp.mods[3], w_router_pad, b_router_pad, cnt_in)


def _swiglu_pairs(u):
    rows, width = u.shape
    lane = lax.broadcasted_iota(I32, (rows, LANES), 1)
    low = lane < LANES // 2
    evens_first = jnp.where(low, 2 * lane, 2 * lane - LANES + 1)
    odds_first = jnp.where(low, 2 * lane + 1, 2 * lane - LANES)
    outs = []
    for m in range(width // (2 * LANES)):
        ua = jnp.take_along_axis(u[:, (2 * m) * LANES:(2 * m + 1) * LANES], evens_first, axis=1)
        ub = jnp.take_along_axis(u[:, (2 * m + 1) * LANES:(2 * m + 2) * LANES], odds_first, axis=1)
        glu_in = jnp.where(low, ua, ub)
        lin_in = pltpu.roll(jnp.where(low, ub, ua), LANES // 2, 1)
        x_glu = jnp.minimum(glu_in, SWIGLU_LIMIT)
        x_lin = jnp.clip(lin_in, -SWIGLU_LIMIT, SWIGLU_LIMIT)
        outs.append(x_glu * (1.0 / (1.0 + jnp.exp(-SWIGLU_ALPHA * x_glu))) * (x_lin + 1.0))
    return jnp.concatenate(outs, axis=1)


def _up_kernel(se_ref, sc_ref, ss_ref, tok_ref, h_ref, w_ref, b_ref, o_ref,
               ids_ref, stage_ref, xbf_ref, wbf_ref, sem_ids, sem_rows):
    p = pl.program_id(0)
    c = pl.program_id(1)
    cnt = sc_ref[p]
    nsub = (cnt + SUB - 1) // SUB

    def row_copy(slot, r, tok):
        return pltpu.make_async_copy(h_ref.at[pl.ds(tok, 1)], stage_ref.at[slot, pl.ds(r, 1)],
                                     sem_rows.at[slot])

    def issue(s):
        slot = s % 2

        def body(r, carry):
            row_copy(slot, r, ids_ref[s * SUB + r]).start()
            return carry
        lax.fori_loop(0, SUB, body, 0, unroll=8)

    def drain(s):
        slot = s % 2

        def body(r, carry):
            row_copy(slot, r, 0).wait()
            return carry
        lax.fori_loop(0, SUB, body, 0, unroll=8)

    @pl.when((cnt > 0) & (c == 0))
    def _():
        ids_copy = pltpu.make_async_copy(
            tok_ref.at[pl.ds(pl.multiple_of(p * SLOT_IDS, ID_ALIGN), SLOT_IDS)], ids_ref, sem_ids)
        ids_copy.start()
        ids_copy.wait()
        issue(0)

        def body(s, carry):
            @pl.when(s + 1 < nsub)
            def _():
                issue(s + 1)
            drain(s)
            r0 = pl.multiple_of(s * SUB, SUB)
            xbf_ref[pl.ds(r0, SUB), :] = stage_ref[s % 2].astype(BF16)
            return carry
        lax.fori_loop(0, nsub, body, 0)

    @pl.when(cnt > 0)
    def _():
        wbf_ref[...] = w_ref[...].astype(BF16)

        def body(s, carry):
            r0 = pl.multiple_of(s * SUB, SUB)
            u = _dot(xbf_ref[pl.ds(r0, SUB), :], wbf_ref[...]) + b_ref[...]
            o_ref[pl.ds(r0, SUB), :] = _swiglu_pairs(u).astype(o_ref.dtype)
            return carry
        lax.fori_loop(0, nsub, body, 0)

        def zbody(s, carry):
            r0 = pl.multiple_of(s * SUB, SUB)
            o_ref[pl.ds(r0, SUB), :] = jnp.zeros((SUB, o_ref.shape[1]), o_ref.dtype)
            return carry
        lax.fori_loop(nsub, SUBS_PER_SLOT, zbody, 0)


def _experts_up(h, tok_pad, slot_e, slot_cnt, slot_src, w1, b1, layer):
    n_slots = slot_e.shape[0]
    d = h.shape[1]
    n2 = w1.shape[3]
    nch = n2 // UP_CHUNK
    hid = UP_CHUNK // 2

    def cidx(p, c, sc):
        return jnp.where(sc[p] > 0, c, nch - 1)

    gs = pltpu.PrefetchScalarGridSpec(
        num_scalar_prefetch=3,
        grid=(n_slots, nch),
        in_specs=[pl.BlockSpec(memory_space=pl.ANY),
                  pl.BlockSpec(memory_space=pl.ANY),
                  pl.BlockSpec((None, None, d, UP_CHUNK), lambda p, c, se, sc, ss: (layer, se[p], 0, cidx(p, c, sc))),
                  pl.BlockSpec((None, None, 1, UP_CHUNK), lambda p, c, se, sc, ss: (layer, se[p], 0, cidx(p, c, sc)))],
        out_specs=pl.BlockSpec((CAP, hid), lambda p, c, se, sc, ss: (ss[p], cidx(p, c, sc))),
        scratch_shapes=[pltpu.SMEM((SLOT_IDS,), I32),
                        pltpu.VMEM((2, SUB, d), F32),
                        pltpu.VMEM((CAP, d), BF16),
                        pltpu.VMEM((d, UP_CHUNK), BF16),
                        pltpu.SemaphoreType.DMA(()),
                        pltpu.SemaphoreType.DMA((2,))])
    return pl.pallas_call(
        _up_kernel, grid_spec=gs,
        out_shape=jax.ShapeDtypeStruct((n_slots * CAP, n2 // 2), BF16),
        compiler_params=_cparams("arbitrary", "arbitrary"),
        name="experts_up",
    )(slot_e, slot_cnt, slot_src, tok_pad, h, w1, b1.reshape(b1.shape[0], b1.shape[1], 1, n2))


def _down_kernel(se_ref, sc_ref, ss_ref, a_ref, w_ref, b_ref, o_ref, wbf_ref):
    p = pl.program_id(0)
    cnt = sc_ref[p]
    nsub = (cnt + SUB - 1) // SUB

    @pl.when(cnt > 0)
    def _():
        wbf_ref[...] = w_ref[...].astype(BF16)

        def body(s, carry):
            r0 = pl.multiple_of(s * SUB, SUB)
            o_ref[pl.ds(r0, SUB), :] = _dot(a_ref[pl.ds(r0, SUB), :], wbf_ref[...]) + b_ref[...]
            return carry
        lax.fori_loop(0, nsub, body, 0)

        def zbody(s, carry):
            r0 = pl.multiple_of(s * SUB, SUB)
            o_ref[pl.ds(r0, SUB), :] = jnp.zeros((SUB, o_ref.shape[1]), o_ref.dtype)
            return carry
        lax.fori_loop(nsub, SUBS_PER_SLOT, zbody, 0)


def _experts_down(a, slot_e, slot_cnt, slot_src, w2, b2, layer):
    n_slots = slot_e.shape[0]
    de = a.shape[1]
    d = w2.shape[3]
    nch = d // DOWN_CHUNK

    def cidx(p, c, sc):
        return jnp.where(sc[p] > 0, c, nch - 1)

    gs = pltpu.PrefetchScalarGridSpec(
        num_scalar_prefetch=3,
        grid=(n_slots, nch),
        in_specs=[pl.BlockSpec((CAP, de), lambda p, c, se, sc, ss: (ss[p], 0)),
                  pl.BlockSpec((None, None, de, DOWN_CHUNK), lambda p, c, se, sc, ss: (layer, se[p], 0, cidx(p, c, sc))),
                  pl.BlockSpec((None, None, 1, DOWN_CHUNK), lambda p, c, se, sc, ss: (layer, se[p], 0, cidx(p, c, sc)))],
        out_specs=pl.BlockSpec((CAP, DOWN_CHUNK), lambda p, c, se, sc, ss: (ss[p], cidx(p, c, sc))),
        scratch_shapes=[pltpu.VMEM((de, DOWN_CHUNK), BF16)])
    return pl.pallas_call(
        _down_kernel, grid_spec=gs,
        out_shape=jax.ShapeDtypeStruct((n_slots * CAP, d), F32),
        compiler_params=_cparams("arbitrary", "arbitrary"),
        name="experts_down",
    )(slot_e, slot_cnt, slot_src, a, w2, b2.reshape(b2.shape[0], b2.shape[1], 1, d))


def _combine_kernel(pos_ref, x_ref, gate_ref, gt_ref, y_ref, o_ref, ids_ref, buf_ref, sem_ids, sem_rows,
                    *, tile0):
    i = pl.program_id(0)
    tc = x_ref.shape[0]
    ids_copy = pltpu.make_async_copy(
        pos_ref.at[pl.ds(pl.multiple_of((i + tile0) * COMB_IDS, ID_ALIGN), COMB_IDS)], ids_ref, sem_ids)
    ids_copy.start()
    ids_copy.wait()

    def row_copy(k, r, src):
        return pltpu.make_async_copy(y_ref.at[pl.ds(src, 1)], buf_ref.at[k, pl.ds(r, 1)], sem_rows)

    def issue(r, carry):
        for k in range(TOP_K):
            row_copy(k, r, ids_ref[k * tc + r]).start()
        return carry
    lax.fori_loop(0, tc, issue, 0, unroll=4)

    def drain(r, carry):
        for k in range(TOP_K):
            row_copy(k, r, 0).wait()
        return carry
    lax.fori_loop(0, tc, drain, 0, unroll=4)

    g = gate_ref[...]
    y = None
    for k in range(TOP_K):
        part = g[:, k:k + 1] * buf_ref[k]
        y = part if y is None else y + part
    o_ref[...] = x_ref[...] + gt_ref[...] * y


def _combine(x, gates, pos_tiles, y_sorted, grp, tile0):
    t, d = x.shape
    tc = COMB_TILE
    return pl.pallas_call(
        functools.partial(_combine_kernel, tile0=tile0),
        grid=(t // tc,),
        in_specs=[pl.BlockSpec(memory_space=pl.ANY),
                  pl.BlockSpec((tc, d), lambda i: (i, 0)),
                  pl.BlockSpec((tc, LANES), lambda i: (i, 0)),
                  grp.mod_spec(5, tile=tc),
                  pl.BlockSpec(memory_space=pl.ANY)],
        out_specs=pl.BlockSpec((tc, d), lambda i: (i, 0)),
        out_shape=jax.ShapeDtypeStruct((t, d), F32),
        scratch_shapes=[pltpu.SMEM((COMB_IDS,), I32),
                        pltpu.VMEM((TOP_K, tc, d), F32),
                        pltpu.SemaphoreType.DMA(()),
                        pltpu.SemaphoreType.DMA(())],
        compiler_params=_cparams("arbitrary"),
        name="moe_combine",
    )(pos_tiles, x, gates, grp.mods[5], y_sorted)


def _slot_tables(counts, n_slots):
    n_exp = counts.shape[0]
    per = (counts + CAP - 1) // CAP
    ends = jnp.cumsum(per)
    base = ends - per
    total = ends[-1]
    p = jnp.arange(n_slots, dtype=I32)
    p_eff = jnp.minimum(p, jnp.maximum(total - 1, 0))
    e = jnp.minimum(jnp.sum((ends[None, :] <= p_eff[:, None]).astype(I32), axis=1), n_exp - 1).astype(I32)
    j = p_eff - base[e]
    cnt = jnp.where(p < total, jnp.clip(counts[e] - j * CAP, 0, CAP), 0).astype(I32)
    return e, cnt, p_eff.astype(I32), base.astype(I32)


def _moe(groups, xs, g_ffn, w_router_pad, b_router_pad, w1, b1, w2, b2, layer, n_slots):
    d = xs[0].shape[1]
    cnt = jnp.zeros((SUBLANES, LANES), I32)
    hs, eids, gates, ranks = [], [], [], []
    for grp, x in zip(groups, xs):
        h, eid, gate, rank, cnt = _route(x, g_ffn, grp, w_router_pad, b_router_pad, cnt)
        hs.append(h)
        eids.append(eid[:, :TOP_K])
        gates.append(gate)
        ranks.append(rank[:, :TOP_K])
    n_exp = w1.shape[1]
    counts = cnt[0, :n_exp]
    slot_e, slot_cnt, slot_src, base = _slot_tables(counts, n_slots)
    h_all = jnp.concatenate(hs, axis=0)
    eid_all = jnp.concatenate(eids, axis=0)
    pos = base[eid_all] * CAP + jnp.concatenate(ranks, axis=0)
    t_all = h_all.shape[0]
    tok = jnp.broadcast_to(jnp.arange(t_all, dtype=I32)[:, None], pos.shape)
    id_pos = pos // CAP * SLOT_IDS + pos % CAP
    tok_pad = jnp.zeros((n_slots * SLOT_IDS,), I32).at[id_pos.reshape(-1)].set(tok.reshape(-1))
    act = _experts_up(h_all, tok_pad, slot_e, slot_cnt, slot_src, w1, b1, layer)
    y_sorted = _experts_down(act, slot_e, slot_cnt, slot_src, w2, b2, layer)
    n_tiles = t_all // COMB_TILE
    pos_tiles = pos.reshape(n_tiles, COMB_TILE, TOP_K).transpose(0, 2, 1).reshape(n_tiles, TOP_K * COMB_TILE)
    pos_tiles = jnp.pad(pos_tiles, ((0, 0), (0, COMB_IDS - TOP_K * COMB_TILE))).reshape(-1)
    outs = []
    tile0 = 0
    for grp, x, gate in zip(groups, xs, gates):
        outs.append(_combine(x, gate, pos_tiles, y_sorted, grp, tile0))
        tile0 += x.shape[0] // COMB_TILE
    return outs


def kernel(x_prompt, x_sample, cache_a_k, cache_a_v, cache_b_k, cache_b_v, c_prompt, c_sample, rel_bias, w_ada, b_ada, g_mix, g_ffn, w_in, w_out, sinks, w_router, b_router, w1, b1, w2, b2, g_final):
    bsz, seq, d = x_prompt.shape
    nseq, s_new, _ = x_sample.shape
    depth = w_ada.shape[0]
    n_heads = d // HEAD_DIM
    nh_a = n_heads // 2
    nh_b = n_heads - nh_a
    nkv_a = nh_a // KV_GROUP
    nkv_b = nh_b // KV_GROUP
    qa_w, ka_w = nh_a * HEAD_DIM, nkv_a * HEAD_DIM
    qb_w, kb_w = nh_b * HEAD_DIM, nkv_b * HEAD_DIM
    buf_a = cache_a_k.shape[2]
    buf_b = cache_b_k.shape[2]
    n_exp = w1.shape[1]
    tp, ts = bsz * seq, nseq * s_new
    assert ts % ROW_TILE == 0 and seq % MM_TM == 0
    n_slots = n_exp + (TOP_K * (tp + ts)) // CAP
    scale = HEAD_DIM ** -0.5

    n_c = bsz + nseq
    r_c = -(-n_c // SUBLANES) * SUBLANES
    c_all = jnp.pad(jnp.concatenate([c_prompt, c_sample], axis=0), ((0, r_c - n_c), (0, 0)))
    mod = _modulation(c_all, w_ada, b_ada).reshape(depth, r_c, 6, d)

    rel_t = rel_bias.T
    heads_a = slice(0, nh_a)
    heads_b = slice(nh_a, n_heads)
    band_a = [_band_table(rel_t, r, heads_a) for _, r in DILATIONS]
    band_b = _band_table(rel_t, 1, heads_b)
    dec_a = _decode_tables(rel_t, heads_a, s_new, buf_a, DILATIONS)
    dec_b = _decode_tables(rel_t, heads_b, s_new, buf_b, ((WIN_B, 1),))

    w_router_pad = jnp.pad(w_router, ((0, 0), (0, 0), (0, LANES - n_exp)))
    b_router_pad = jnp.pad(b_router, ((0, 0), (0, LANES - n_exp)), constant_values=MASKED)

    cache_a_k2 = cache_a_k.reshape(depth, nseq, buf_a, ka_w)
    cache_a_v2 = cache_a_v.reshape(depth, nseq, buf_a, ka_w)
    cache_b_k2 = cache_b_k.reshape(depth, nseq, buf_b, kb_w)
    cache_b_v2 = cache_b_v.reshape(depth, nseq, buf_b, kb_w)

    xp = x_prompt.reshape(tp, d)
    xs = x_sample.reshape(ts, d)
    rows_p, rows_s = [], []
    col = [0, qa_w, qa_w + ka_w, qa_w + 2 * ka_w, 2 * qa_w + 2 * ka_w, 2 * qa_w + 2 * ka_w + kb_w]

    for l in range(depth):
        mods_p = [mod[l, :bsz, k][:, None, :] for k in range(6)]
        mods_s = [jnp.repeat(mod[l, bsz:bsz + nseq, k], s_new, axis=0)[None] for k in range(6)]
        gp = _Group(tp, ROW_TILE, mods_p, seq // ROW_TILE)
        gsm = _Group(ts, ROW_TILE, mods_s, ts // ROW_TILE)

        def project(x, grp, tm):
            h = _norm_mod(x, g_mix[l], grp, 1, 0, BF16)
            qa = _matmul_ws([h], w_in, l, col[0], qa_w, BF16, tm, scale=scale)
            ka = _matmul_ws([h], w_in, l, col[1], ka_w, F32, tm)
            va = _matmul_ws([h], w_in, l, col[2], ka_w, F32, tm)
            qb = _matmul_ws([h], w_in, l, col[3], qb_w, BF16, tm, scale=scale)
            kb = _matmul_ws([h], w_in, l, col[4], kb_w, F32, tm)
            vb = _matmul_ws([h], w_in, l, col[5], kb_w, F32, tm)
            return qa, ka, va, qb, kb, vb

        qa, ka, va, qb, kb, vb = project(xp, gp, MM_TM)
        q3, k3, v3 = qa.reshape(bsz, seq, qa_w), ka.reshape(bsz, seq, ka_w), va.reshape(bsz, seq, ka_w)
        outs, lses = [], []
        for (_, r), tab in zip(DILATIONS, band_a):
            o, lse = _band_attention(q3, k3, v3, tab, r)
            outs.append(o)
            lses.append(lse)
        oa = _mix_branches(outs, lses).reshape(tp, qa_w)
        ob = _band_attention(qb.reshape(bsz, seq, qb_w), kb.reshape(bsz, seq, kb_w), vb.reshape(bsz, seq, kb_w),
                             band_b, 1, sinks=sinks[l], emit_lse=False).reshape(tp, qb_w)
        xp = _matmul_ws([oa, ob], w_out, l, 0, d, F32, MM_TM, res=xp, grp=gp, gate_k=2)
        rows_p.append((ka.reshape(bsz, seq, nkv_a, HEAD_DIM)[:, seq - min(buf_a, seq):],
                       va.reshape(bsz, seq, nkv_a, HEAD_DIM)[:, seq - min(buf_a, seq):],
                       kb.reshape(bsz, seq, nkv_b, HEAD_DIM)[:, seq - min(buf_b, seq):],
                       vb.reshape(bsz, seq, nkv_b, HEAD_DIM)[:, seq - min(buf_b, seq):]))

        qa, ka, va, qb, kb, vb = project(xs, gsm, ROW_TILE)
        sink_col = jnp.broadcast_to(jnp.repeat(sinks[l], s_new)[:, None], (nh_b * s_new, LANES))
        oa = _decode_attention(qa.reshape(nseq, s_new, qa_w), cache_a_k2, cache_a_v2, l,
                               ka.reshape(nseq, s_new, ka_w), va.reshape(nseq, s_new, ka_w), *dec_a)
        ob = _decode_attention(qb.reshape(nseq, s_new, qb_w), cache_b_k2, cache_b_v2, l,
                               kb.reshape(nseq, s_new, kb_w), vb.reshape(nseq, s_new, kb_w), *dec_b,
                               sink_col=sink_col)
        xs = _matmul_ws([oa.reshape(ts, qa_w).astype(BF16), ob.reshape(ts, qb_w).astype(BF16)],
                        w_out, l, 0, d, F32, ROW_TILE, res=xs, grp=gsm, gate_k=2)
        rows_s.append((ka.reshape(nseq, s_new, nkv_a, HEAD_DIM), va.reshape(nseq, s_new, nkv_a, HEAD_DIM),
                       kb.reshape(nseq, s_new, nkv_b, HEAD_DIM), vb.reshape(nseq, s_new, nkv_b, HEAD_DIM)))

        xp, xs = _moe([gp, gsm], [xp, xs], g_ffn[l], w_router_pad[l], b_router_pad[l][None, :],
                      w1, b1, w2, b2, l, n_slots)

    y_prompt = _norm_plain(xp, g_final, ROW_TILE).reshape(bsz, seq, d)
    y_sample = _norm_plain(xs, g_final, ROW_TILE).reshape(nseq, s_new, d)
    prompt_a_k, prompt_a_v, prompt_b_k, prompt_b_v = [jnp.stack(r) for r in zip(*rows_p)]
    new_a_k, new_a_v, new_b_k, new_b_v = [jnp.stack(r) for r in zip(*rows_s)]

    def roll_cache(cache, new, win):
        both = cache.shape[2] + new.shape[2]
        keep = min(win, both)
        return jnp.concatenate([cache[:, :, both - keep:], new], axis=2)

    win_a = max(w for w, _ in DILATIONS)
    sample_a_k = roll_cache(cache_a_k, new_a_k, win_a)
    sample_a_v = roll_cache(cache_a_v, new_a_v, win_a)
    sample_b_k = roll_cache(cache_b_k, new_b_k, WIN_B)
    sample_b_v = roll_cache(cache_b_v, new_b_v, WIN_B)
    return (y_prompt, y_sample, prompt_a_k, prompt_a_v, prompt_b_k, prompt_b_v,
            sample_a_k, sample_a_v, sample_b_k, sample_b_v)
```
